```python
import math
import numpy as np
import jax
import jax.numpy as jnp
from jax import lax

D_MODEL = 1024
BATCH = 8
SEQ = 4096
DEPTH = 4

EPS = 1e-6
ROPE_THETA = 10000.0
MASK_VALUE = -1e30

DN_HEADS = 4
DN_DK = 128
DN_DV = 128
DN_CONV = 4
DN_CHUNK = 64
SC_CHANNELS = 512
SC_CONV = 3
HG_HEADS = 4
HG_DK = 128
HG_DV = 128
HG_CHUNK = 64
SWA_PATTERNS = ((128, 1), (512, 4), (2048, 16))
SWA_HEADS = 4
SWA_DH = 128
FFN_HIDDEN = -(-8 * D_MODEL // (3 * 256)) * 256

DN_W = DN_HEADS * DN_DK
DN_VW = DN_HEADS * DN_DV
AB_SPLITS = (2 * DN_W + DN_VW, DN_VW, DN_HEADS, DN_HEADS, SC_CHANNELS, SC_CHANNELS, SC_CHANNELS)
AB_IN = sum(AB_SPLITS)
AB_OUT = DN_VW + SC_CHANNELS
HG_W = HG_HEADS * HG_DK
HG_VW = HG_HEADS * HG_DV
SWA_W = len(SWA_PATTERNS) * SWA_HEADS * SWA_DH
CD_SPLITS = (HG_W, HG_W, HG_VW, HG_VW, SWA_W, SWA_W, SWA_W)
CD_IN = sum(CD_SPLITS)
CD_OUT = HG_VW + SWA_HEADS * SWA_DH
N_EVEN = (DEPTH + 1) // 2
N_ODD = DEPTH // 2

kernel_name = 'hybrid_deltanet_shortconv_hgrn2_dilated_swa'


def rms_norm(x, g):
    xf = x.astype(jnp.float32)
    y = xf * lax.rsqrt(jnp.mean(xf * xf, axis=-1, keepdims=True) + EPS)
    return (y * g.astype(jnp.float32)).astype(x.dtype)


def l2_normalize(x):
    xf = x.astype(jnp.float32)
    return xf * lax.rsqrt(jnp.sum(xf * xf, axis=-1, keepdims=True) + EPS)


def split_cols(a, sizes):
    return jnp.split(a, [int(s) for s in np.cumsum(sizes)[:-1]], axis=-1)


def causal_depthwise_conv(x, w):
    width = w.shape[0]
    t = x.shape[1]
    xp = jnp.pad(x, ((0, 0), (width - 1, 0), (0, 0)))
    return sum(xp[:, j:j + t] * w[j] for j in range(width))


def masked_exp(mask, logits):
    return jnp.where(mask, jnp.exp(jnp.where(mask, logits, 0.0)), 0.0)


def rope(x, pos):
    dh = x.shape[-1]
    half = dh // 2
    inv_freq = ROPE_THETA ** (-jnp.arange(half, dtype=jnp.float32) * 2.0 / dh)
    ang = pos.astype(jnp.float32)[:, None] * inv_freq[None, :]
    shape = (1, x.shape[1]) + (1,) * (x.ndim - 3) + (half,)
    cos = jnp.cos(ang).reshape(shape)
    sin = jnp.sin(ang).reshape(shape)
    xf = x.astype(jnp.float32)
    x1, x2 = xf[..., :half], xf[..., half:]
    return jnp.concatenate([x1 * cos - x2 * sin, x1 * sin + x2 * cos], axis=-1).astype(x.dtype)


def to_chunks(a, c):
    b, t = a.shape[:2]
    a = a.reshape((b, t // c, c) + a.shape[2:])
    return jnp.moveaxis(jnp.moveaxis(a, 1, 0), 2, 3)


def from_chunks(a):
    a = jnp.moveaxis(jnp.moveaxis(a, 3, 2), 0, 1)
    n_b, n, c, h, d = a.shape
    return a.reshape(n_b, n * c, h, d)


def gated_delta_rule(q, k, v, g, beta):
    f32 = jnp.float32
    b, t, h, dk = q.shape
    dv = v.shape[-1]
    c = DN_CHUNK
    q, k, v = (to_chunks(a.astype(f32), c) for a in (q, k, v))
    gc = jnp.cumsum(to_chunks(g.astype(f32), c), axis=-1)
    bt = to_chunks(beta.astype(f32), c)
    causal = jnp.tril(jnp.ones((c, c), bool))
    decay = masked_exp(causal, gc[..., :, None] - gc[..., None, :])
    kk = jnp.einsum('nbhcd,nbhsd->nbhcs', k, k)
    a_strict = jnp.where(jnp.eye(c, dtype=bool), 0.0, bt[..., None] * kk * decay)
    rhs = jnp.concatenate([v * bt[..., None], k * (bt * jnp.exp(gc))[..., None]], axis=-1)
    sol = lax.linalg.triangular_solve(a_strict + jnp.eye(c, dtype=f32), rhs,
                                      left_side=True, lower=True, unit_diagonal=True)
    u_base, w = sol[..., :dv], sol[..., dv:]
    qk = jnp.einsum('nbhcd,nbhsd->nbhcs', q, k) * decay
    q_dec = q * jnp.exp(gc)[..., None]
    k_dec = k * jnp.exp(gc[..., -1:] - gc)[..., None]
    g_tot = jnp.exp(gc[..., -1])

    def step(state, inp):
        u_b, w_c, qk_c, qd_c, kd_c, gt_c = inp
        u = u_b - jnp.einsum('bhcd,bhde->bhce', w_c, state)
        o = jnp.einsum('bhcd,bhde->bhce', qd_c, state) + jnp.einsum('bhcs,bhse->bhce', qk_c, u)
        state = state * gt_c[..., None, None] + jnp.einsum('bhcd,bhce->bhde', kd_c, u)
        return state, o

    s0 = jnp.zeros((b, h, dk, dv), f32)
    _, o = lax.scan(step, s0, (u_base, w, qk, q_dec, k_dec, g_tot))
    return from_chunks(o)


def hgrn2_recurrence(q, k, v, log_f):
    f32 = jnp.float32
    b, t, h, dk = q.shape
    dv = v.shape[-1]
    c = HG_CHUNK
    q, k, v, lf = (to_chunks(a.astype(f32), c) for a in (q, k, v, log_f))
    bcum = jnp.cumsum(lf, axis=3)
    causal = jnp.tril(jnp.ones((c, c), bool))[:, :, None]

    def step(state, inp):
        q_c, k_c, v_c, b_c = inp
        decay = masked_exp(causal, b_c[:, :, :, None, :] - b_c[:, :, None, :, :])
        attn = jnp.einsum('bhtsd,bhsd->bhts', decay * q_c[:, :, :, None, :], k_c)
        o = (jnp.einsum('bhts,bhse->bhte', attn, v_c)
             + jnp.einsum('bhtd,bhde->bhte', q_c * jnp.exp(b_c), state))
        b_last = b_c[:, :, -1:, :]
        state = (state * jnp.exp(b_last[:, :, 0, :, None])
                 + jnp.einsum('bhsd,bhse->bhde', k_c * jnp.exp(b_last - b_c), v_c))
        return state, o

    s0 = jnp.zeros((b, h, dk, dv), f32)
    _, o = lax.scan(step, s0, (q, k, v, bcum))
    return from_chunks(o)


def dilated_window_attention(q, k, v, window, dilation):
    f32 = jnp.float32
    b, t, h, dh = q.shape
    span = window // dilation
    length = t // dilation
    nb = -(-length // span)
    pad = nb * span - length

    def blocks(a):
        a = a.astype(f32).reshape(b, length, dilation, h, dh).transpose(0, 2, 3, 1, 4)
        a = jnp.pad(a, ((0, 0), (0, 0), (0, 0), (0, pad), (0, 0)))
        return a.reshape(b, dilation, h, nb, span, dh)

    def with_prev(a):
        prev = jnp.pad(a, ((0, 0), (0, 0), (0, 0), (1, 0), (0, 0), (0, 0)))[:, :, :, :-1]
        return jnp.concatenate([prev, a], axis=4)

    qb = blocks(q)
    kb = with_prev(blocks(k))
    vb = with_prev(blocks(v))
    s = jnp.einsum('brhnqd,brhnkd->brhnqk', qb, kb) * (dh ** -0.5)
    qi = jnp.arange(span)[:, None]
    kj = jnp.arange(2 * span)[None, :]
    dist = qi + span - kj
    in_range = jnp.arange(nb)[:, None, None] * span + kj - span >= 0
    valid = (dist >= 0) & (dist <= span) & in_range
    s = jnp.where(valid, s, MASK_VALUE)
    m = jnp.max(s, axis=-1, keepdims=True)
    p = jnp.where(valid, jnp.exp(s - m), 0.0)
    l = jnp.sum(p, axis=-1, keepdims=True)
    o = jnp.einsum('brhnqk,brhnkd->brhnqd', p / l, vb)
    lse = (m + jnp.log(l))[..., 0]
    o = o.reshape(b, dilation, h, nb * span, dh)[:, :, :, :length].transpose(0, 3, 1, 2, 4).reshape(b, t, h, dh)
    lse = lse.reshape(b, dilation, h, nb * span)[..., :length].transpose(0, 3, 1, 2).reshape(b, t, h)
    return o, lse


def mixer_ab(h, w_in, conv_w, a_log, dt_bias, norm_g, sc_conv_w, w_out):
    f32 = jnp.float32
    b, t, _ = h.shape
    qkv, z, beta_raw, alpha_raw, gate_b, gate_c, sc_in = split_cols(h @ w_in, AB_SPLITS)
    qkv = jax.nn.silu(causal_depthwise_conv(qkv, conv_w))
    q, k, v = split_cols(qkv, (DN_W, DN_W, DN_VW))
    q = l2_normalize(q.reshape(b, t, DN_HEADS, DN_DK)) * (DN_DK ** -0.5)
    k = l2_normalize(k.reshape(b, t, DN_HEADS, DN_DK))
    v = v.reshape(b, t, DN_HEADS, DN_DV)
    beta = jax.nn.sigmoid(beta_raw.astype(f32))
    g = -jnp.exp(a_log.astype(f32)) * jax.nn.softplus(alpha_raw.astype(f32) + dt_bias.astype(f32))
    o_a = gated_delta_rule(q, k, v, g, beta)
    o_a = (rms_norm(o_a, norm_g) * jax.nn.silu(z.reshape(b, t, DN_HEADS, DN_DV))).reshape(b, t, DN_VW)
    o_b = gate_b * causal_depthwise_conv(gate_c * sc_in, sc_conv_w)
    return jnp.concatenate([o_a.astype(h.dtype), o_b.astype(h.dtype)], axis=-1) @ w_out


def mixer_cd(h, w_in, lower_bound, norm_g, w_out):
    f32 = jnp.float32
    b, t, _ = h.shape
    hq, hf, hi, hg, sq, sk, sv = split_cols(h @ w_in, CD_SPLITS)
    lb = lower_bound.astype(f32)
    hf = hf.astype(f32)
    f_gate = lb + (1.0 - lb) * jax.nn.sigmoid(hf)
    log_f = jnp.log(f_gate)
    k_in = (1.0 - lb) * jax.nn.sigmoid(-hf)
    hgrn_heads = lambda a, d: a.reshape(b, t, HG_HEADS, d)
    o_c = hgrn2_recurrence(hgrn_heads(jax.nn.silu(hq), HG_DK), hgrn_heads(k_in, HG_DK),
                           hgrn_heads(hi, HG_DV), hgrn_heads(log_f, HG_DK))
    o_c = (rms_norm(o_c, norm_g) * jax.nn.sigmoid(hgrn_heads(hg, HG_DV).astype(f32))).reshape(b, t, HG_VW)
    n_pat = len(SWA_PATTERNS)
    pos = jnp.arange(t)
    sq = rope(sq.reshape(b, t, n_pat, SWA_HEADS, SWA_DH), pos)
    sk = rope(sk.reshape(b, t, n_pat, SWA_HEADS, SWA_DH), pos)
    sv = sv.reshape(b, t, n_pat, SWA_HEADS, SWA_DH)
    outs, lses = [], []
    for p, (window, dilation) in enumerate(SWA_PATTERNS):
        o_p, lse_p = dilated_window_attention(sq[:, :, p], sk[:, :, p], sv[:, :, p], window, dilation)
        outs.append(o_p)
        lses.append(lse_p)
    wts = jax.nn.softmax(jnp.stack(lses, axis=-1), axis=-1)
    o_d = jnp.einsum('bthpd,bthp->bthd', jnp.stack(outs, axis=3), wts).reshape(b, t, SWA_HEADS * SWA_DH)
    return jnp.concatenate([o_c.astype(h.dtype), o_d.astype(h.dtype)], axis=-1) @ w_out


def swiglu(h, w_gate, w_up, w_down):
    return (jax.nn.silu(h @ w_gate) * (h @ w_up)) @ w_down


def setup_inputs(seed: int = 0) -> dict:
    key = jax.random.key(seed)
    keys = iter(jax.random.split(key, 24))
    f32 = jnp.float32

    def dense(shape, fan_in):
        return jax.random.normal(next(keys), shape, f32) * (fan_in ** -0.5)

    def gain(shape):
        return 1.0 + 0.02 * jax.random.normal(next(keys), shape, f32)

    x = jax.random.normal(next(keys), (BATCH, SEQ, D_MODEL), f32)
    norm_mix_g = gain((DEPTH, D_MODEL))
    norm_ffn_g = gain((DEPTH, D_MODEL))
    norm_final_g = gain((D_MODEL,))
    ab_w_in = dense((N_EVEN, D_MODEL, AB_IN), D_MODEL)
    dn_conv_w = dense((N_EVEN, DN_CONV, 2 * DN_W + DN_VW), DN_CONV)
    dn_a_log = jnp.log(jax.random.uniform(next(keys), (N_EVEN, DN_HEADS), f32, 1.0, 16.0))
    dt = jnp.exp(jax.random.uniform(next(keys), (N_EVEN, DN_HEADS), f32, math.log(1e-3), math.log(1e-1)))
    dn_dt_bias = dt + jnp.log(-jnp.expm1(-dt))
    dn_norm_g = gain((N_EVEN, DN_DV))
    sc_conv_w = dense((N_EVEN, SC_CONV, SC_CHANNELS), SC_CONV)
    ab_w_out = dense((N_EVEN, AB_OUT, D_MODEL), AB_OUT)
    cd_w_in = dense((N_ODD, D_MODEL, CD_IN), D_MODEL)
    hg_lower_bounds = 0.1 * jax.random.normal(next(keys), (N_ODD, HG_W), f32)
    hg_norm_g = gain((N_ODD, HG_DV))
    cd_w_out = dense((N_ODD, CD_OUT, D_MODEL), CD_OUT)
    ffn_w_gate = dense((DEPTH, D_MODEL, FFN_HIDDEN), D_MODEL)
    ffn_w_up = dense((DEPTH, D_MODEL, FFN_HIDDEN), D_MODEL)
    ffn_w_down = dense((DEPTH, FFN_HIDDEN, D_MODEL), FFN_HIDDEN)
    return {'x': x, 'norm_mix_g': norm_mix_g, 'norm_ffn_g': norm_ffn_g, 'norm_final_g': norm_final_g,
            'ab_w_in': ab_w_in, 'dn_conv_w': dn_conv_w, 'dn_a_log': dn_a_log, 'dn_dt_bias': dn_dt_bias,
            'dn_norm_g': dn_norm_g, 'sc_conv_w': sc_conv_w, 'ab_w_out': ab_w_out,
            'cd_w_in': cd_w_in, 'hg_lower_bounds': hg_lower_bounds, 'hg_norm_g': hg_norm_g,
            'cd_w_out': cd_w_out, 'ffn_w_gate': ffn_w_gate, 'ffn_w_up': ffn_w_up, 'ffn_w_down': ffn_w_down}


def reference(x, norm_mix_g, norm_ffn_g, norm_final_g, ab_w_in, dn_conv_w, dn_a_log, dn_dt_bias,
              dn_norm_g, sc_conv_w, ab_w_out, cd_w_in, hg_lower_bounds, hg_norm_g, cd_w_out,
              ffn_w_gate, ffn_w_up, ffn_w_down):
    sm = jax.nn.softmax(hg_lower_bounds.astype(jnp.float32), axis=0)
    lower_bounds = jnp.cumsum(sm, axis=0) - sm[0]
    h = x
    for layer in range(DEPTH):
        i = layer // 2
        hn = rms_norm(h, norm_mix_g[layer])
        if layer % 2 == 0:
            h = h + mixer_ab(hn, ab_w_in[i], dn_conv_w[i], dn_a_log[i], dn_dt_bias[i], dn_norm_g[i],
                             sc_conv_w[i], ab_w_out[i])
        else:
            h = h + mixer_cd(hn, cd_w_in[i], lower_bounds[i], hg_norm_g[i], cd_w_out[i])
        hn = rms_norm(h, norm_ffn_g[layer])
        h = h + swiglu(hn, ffn_w_gate[layer], ffn_w_up[layer], ffn_w_down[layer])
    return rms_norm(h, norm_final_g)
```

```python
import functools

import jax
import jax.numpy as jnp
from jax import lax
from jax.experimental import pallas as pl
from jax.experimental.pallas import tpu as pltpu

F32 = jnp.float32
BF16 = jnp.bfloat16

D_MODEL = 1024
EPS = 1e-6
ROPE_THETA = 10000.0
MASK_VALUE = -1e30

N_HEADS = 4
HEAD = 128
HW = N_HEADS * HEAD
CHUNK = 64
UNIT = 2 * CHUNK
SUB = 16
DN_CONV = 4
SC_CONV = 3
SWA_PATTERNS = ((128, 1), (512, 4), (2048, 16))
SPAN = 128
FFN_HIDDEN = 2816

VMEM_LIMIT = 56 * 1024 * 1024
SUBLANES = 8

HIGHEST = lax.Precision.HIGHEST


def _cparams(*sem):
    return pltpu.CompilerParams(dimension_semantics=sem, vmem_limit_bytes=VMEM_LIMIT)


def _dot(a, b):
    return jnp.dot(a, b, preferred_element_type=F32)


def _dot_nt(a, b):
    return lax.dot_general(a, b, (((1,), (1,)), ((), ())), preferred_element_type=F32)


def _sigmoid(x):
    return 1.0 / (1.0 + jnp.exp(-x))


def _silu(x):
    return x * _sigmoid(x)


def _iota(shape, dim):
    return lax.broadcasted_iota(jnp.int32, shape, dim)


def _col(x, idx):
    return jnp.sum(jnp.where(_iota(x.shape, 1) == idx, x, 0.0), axis=1, keepdims=True)


def _row(x, idx):
    return jnp.sum(jnp.where(_iota(x.shape, 0) == idx, x, 0.0), axis=0, keepdims=True)


def _rms(x, g):
    ms = jnp.mean(x * x, axis=-1, keepdims=True)
    return x * lax.rsqrt(ms + EPS) * g


def _nm_body(h_ref, g_ref, w_ref, o_ref, hn_ref):
    @pl.when(pl.program_id(1) == 0)
    def _():
        hn_ref[...] = _rms(h_ref[...], g_ref[...]).astype(BF16)

    o_ref[...] = _dot(hn_ref[...], w_ref[...]).astype(o_ref.dtype)


def norm_matmul(h, g, w, out_dtype, tm, tn):
    n, d = h.shape
    nc = w.shape[1]
    return pl.pallas_call(
        _nm_body,
        grid=(n // tm, nc // tn),
        in_specs=[pl.BlockSpec((tm, d), lambda i, j: (i, 0)),
                  pl.BlockSpec((1, d), lambda i, j: (0, 0)),
                  pl.BlockSpec((d, tn), lambda i, j: (0, j))],
        out_specs=pl.BlockSpec((tm, tn), lambda i, j: (i, j)),
        out_shape=jax.ShapeDtypeStruct((n, nc), out_dtype),
        scratch_shapes=[pltpu.VMEM((tm, d), BF16)],
        compiler_params=_cparams("parallel", "arbitrary"),
        name="norm_matmul",
    )(h, g.reshape(1, d), w)


def _ffn_body(h_ref, g_ref, wg_ref, wu_ref, wd_ref, gf_ref, o_ref, *, th, final):
    x = h_ref[...]
    hn = _rms(x, g_ref[...]).astype(BF16)
    acc = x
    for c in range(FFN_HIDDEN // th):
        cs = slice(c * th, (c + 1) * th)
        a = _silu(_dot(hn, wg_ref[:, cs])) * _dot(hn, wu_ref[:, cs])
        acc = acc + _dot(a.astype(BF16), wd_ref[cs, :])
    if final:
        acc = _rms(acc, gf_ref[...])
    o_ref[...] = acc


def ffn(h, g, wg, wu, wd, gf, final, tm=512, th=1408):
    n, d = h.shape
    const = lambda i: (0, 0)
    return pl.pallas_call(
        functools.partial(_ffn_body, th=th, final=final),
        grid=(n // tm,),
        in_specs=[pl.BlockSpec((tm, d), lambda i: (i, 0)),
                  pl.BlockSpec((1, d), const),
                  pl.BlockSpec((d, FFN_HIDDEN), const, pipeline_mode=pl.Buffered(1)),
                  pl.BlockSpec((d, FFN_HIDDEN), const, pipeline_mode=pl.Buffered(1)),
                  pl.BlockSpec((FFN_HIDDEN, d), const, pipeline_mode=pl.Buffered(1)),
                  pl.BlockSpec((1, d), const)],
        out_specs=pl.BlockSpec((tm, d), lambda i: (i, 0)),
        out_shape=jax.ShapeDtypeStruct((n, d), F32),
        compiler_params=_cparams("parallel"),
        name="ffn",
    )(h, g.reshape(1, d), wg, wu, wd, gf.reshape(1, d))


def _dprep_body(x_ref, halo_ref, ba_ref, cw_ref, alog_ref, dtb_ref,
                ub_ref, w_ref, qd_ref, kdt_ref, qk_ref, gt_ref, xs_ref, *, tt):
    first = (pl.program_id(1) == 0)
    keep = jnp.where(first, 0.0, 1.0)
    xs_ref[0:SUBLANES, :] = halo_ref[...].astype(F32) * keep
    xs_ref[SUBLANES:SUBLANES + tt, :] = x_ref[...].astype(F32)
    cw = cw_ref[...]
    y = None
    for j in range(DN_CONV):
        term = xs_ref[pl.ds(SUBLANES - (DN_CONV - 1) + j, tt), :] * cw[j:j + 1, :]
        y = term if y is None else y + term
    y = _silu(y)

    ba = ba_ref[...]
    sp_in = ba + dtb_ref[...]
    softplus = jnp.maximum(sp_in, 0.0) + jnp.log1p(jnp.exp(-jnp.abs(sp_in)))
    g_all = -jnp.exp(alog_ref[...]) * softplus
    beta_all = _sigmoid(ba)

    r = _iota((UNIT, UNIT), 0)
    c = _iota((UNIT, UNIT), 1)
    same = (r // CHUNK) == (c // CHUNK)
    m_lo = same & (c <= r)
    m_strict = same & (c < r)
    tril_f = jnp.where(m_lo, 1.0, 0.0)
    eye_f = jnp.where(r == c, 1.0, 0.0)
    first_half = _iota((UNIT, HEAD), 0) < CHUNK

    for u in range(tt // UNIT):
        rows = slice(u * UNIT, (u + 1) * UNIT)
        gcol_all = jnp.dot(tril_f, g_all[rows], precision=HIGHEST, preferred_element_type=F32)
        grow_all = gcol_all.T
        last0 = _row(gcol_all, CHUNK - 1)
        last1 = _row(gcol_all, UNIT - 1)
        glast_all = jnp.where(first_half, last0, last1)
        ecol_all = jnp.exp(gcol_all)
        edec_all = jnp.exp(glast_all - gcol_all)
        gt0_all = jnp.exp(last0)
        gt1_all = jnp.exp(last1)
        beta_u = beta_all[rows]
        for h in range(N_HEADS):
            cs = slice(h * HEAD, (h + 1) * HEAD)
            q = y[rows, h * HEAD:(h + 1) * HEAD]
            k = y[rows, HW + h * HEAD:HW + (h + 1) * HEAD]
            v = y[rows, 2 * HW + h * HEAD:2 * HW + (h + 1) * HEAD]
            q = q * lax.rsqrt(jnp.sum(q * q, axis=-1, keepdims=True) + EPS) * (HEAD ** -0.5)
            k = k * lax.rsqrt(jnp.sum(k * k, axis=-1, keepdims=True) + EPS)
            bcol = _col(beta_u, h)
            gc = _col(gcol_all, N_HEADS + h)
            gr = _row(grow_all, N_HEADS + h)
            ec = _col(ecol_all, N_HEADS + h)
            ed = _col(edec_all, N_HEADS + h)
            dec = jnp.where(m_lo, jnp.exp(jnp.where(m_lo, gc - gr, 0.0)), 0.0)
            kb = k.astype(BF16)
            kk = _dot_nt(kb, kb)
            qk = _dot_nt(q.astype(BF16), kb) * dec
            a = jnp.where(m_strict, bcol * kk * dec, 0.0)
            p = eye_f - a
            xp = a.astype(BF16)
            n_sq = CHUNK.bit_length() - 2
            for it in range(n_sq):
                x2 = _dot(xp, xp)
                p = p + _dot(p.astype(BF16), x2.astype(BF16))
                xp = x2.astype(BF16)
            rhs = jnp.concatenate([v * bcol, k * (bcol * ec)], axis=1).astype(BF16)
            sol = _dot(p.astype(BF16), rhs)
            ub_ref[rows, cs] = sol[:, :HEAD].astype(BF16)
            w_ref[rows, cs] = sol[:, HEAD:].astype(BF16)
            qd_ref[rows, cs] = (q * ec).astype(BF16)
            kdt_ref[rows, cs] = (k * ed).T.astype(BF16)
            qk_ref[rows, cs] = qk.astype(BF16)
            gt_ref[2 * u:2 * u + 1, cs] = jnp.broadcast_to(_col(gt0_all, N_HEADS + h), (1, HEAD))
            gt_ref[2 * u + 1:2 * u + 2, cs] = jnp.broadcast_to(_col(gt1_all, N_HEADS + h), (1, HEAD))


def delta_prep(proj, ba, conv_w, alog_row, dtb_row, bsz, t, tt=512):
    n = proj.shape[0]
    nt = t // tt
    hb = tt // SUBLANES
    row = lambda b, i: (b * nt + i, 0)
    const = lambda b, i: (0, 0)
    big = jax.ShapeDtypeStruct((n, HW), BF16)
    return pl.pallas_call(
        functools.partial(_dprep_body, tt=tt),
        grid=(bsz, nt),
        in_specs=[pl.BlockSpec((tt, 3 * HW), row),
                  pl.BlockSpec((SUBLANES, 3 * HW), lambda b, i: (jnp.maximum((b * nt + i) * hb - 1, 0), 0)),
                  pl.BlockSpec((tt, HEAD), row),
                  pl.BlockSpec((DN_CONV, 3 * HW), const),
                  pl.BlockSpec((1, HEAD), const),
                  pl.BlockSpec((1, HEAD), const)],
        out_specs=[pl.BlockSpec((tt, HW), row)] * 5 + [pl.BlockSpec((tt // CHUNK, HW), row)],
        out_shape=[big] * 5 + [jax.ShapeDtypeStruct((n // CHUNK, HW), F32)],
        scratch_shapes=[pltpu.VMEM((tt + SUBLANES, 3 * HW), F32)],
        compiler_params=_cparams("parallel", "parallel"),
        name="delta_prep",
    )(proj, proj, ba, conv_w, alog_row, dtb_row)


def _dscan_body(ub_ref, w_ref, qd_ref, kdt_ref, qk_ref, gt_ref, z_ref, ng_ref, o_ref, s_ref, *, tt):
    @pl.when(pl.program_id(1) == 0)
    def _():
        s_ref[...] = jnp.zeros_like(s_ref)

    ng = ng_ref[...]
    for u in range(tt // UNIT):
        base = u * UNIT
        u0 = [None] * N_HEADS
        for half in range(2):
            rows = slice(base + half * CHUNK, base + (half + 1) * CHUNK)
            for h in range(N_HEADS):
                cs = slice(h * HEAD, (h + 1) * HEAD)
                s = s_ref[h]
                wq = jnp.concatenate([w_ref[rows, cs], qd_ref[rows, cs]], axis=0)
                rr = _dot(wq, s.astype(BF16))
                uc = ub_ref[rows, cs].astype(F32) - rr[:CHUNK]
                zero = jnp.zeros_like(uc)
                if half == 0:
                    u0[h] = uc
                    u_all = jnp.concatenate([uc, zero], axis=0).astype(BF16)
                    u_sel = u_all
                else:
                    u_all = jnp.concatenate([u0[h], uc], axis=0).astype(BF16)
                    u_sel = jnp.concatenate([zero, uc], axis=0).astype(BF16)
                o = rr[CHUNK:] + _dot(qk_ref[rows, cs], u_all)
                gt = gt_ref[2 * u + half:2 * u + half + 1, cs]
                s_ref[h] = s * gt + _dot(kdt_ref[base:base + UNIT, cs], u_sel)
                on = _rms(o, ng) * _silu(z_ref[rows, cs].astype(F32))
                o_ref[rows, cs] = on.astype(BF16)


def delta_scan(ub, w, qd, kdt, qk, gt, proj, ng, bsz, t, tt=512):
    n = ub.shape[0]
    nt = t // tt
    row = lambda b, i: (b * nt + i, 0)
    blk = pl.BlockSpec((tt, HW), row)
    return pl.pallas_call(
        functools.partial(_dscan_body, tt=tt),
        grid=(bsz, nt),
        in_specs=[blk] * 5 + [pl.BlockSpec((tt // CHUNK, HW), row),
                              pl.BlockSpec((tt, HW), lambda b, i: (b * nt + i, 3)),
                              pl.BlockSpec((1, HEAD), lambda b, i: (0, 0))],
        out_specs=blk,
        out_shape=jax.ShapeDtypeStruct((n, HW), BF16),
        scratch_shapes=[pltpu.VMEM((N_HEADS, HEAD, HEAD), F32)],
        compiler_params=_cparams("parallel", "arbitrary"),
        name="delta_scan",
    )(ub, w, qd, kdt, qk, gt, proj, ng)


def _evenout_body(oa_ref, gb_ref, gc_ref, sc_ref, gch_ref, sch_ref, scw_ref, wo_ref, h_ref, o_ref, ps_ref,
                  *, tm, tiles_per_seq):
    first = (pl.program_id(0) % tiles_per_seq) == 0
    keep = jnp.where(first, 0.0, 1.0)
    ps_ref[0:SUBLANES, :] = gch_ref[...].astype(F32) * sch_ref[...].astype(F32) * keep
    ps_ref[SUBLANES:SUBLANES + tm, :] = gc_ref[...].astype(F32) * sc_ref[...].astype(F32)
    scw = scw_ref[...]
    y = None
    for j in range(SC_CONV):
        term = ps_ref[pl.ds(SUBLANES - (SC_CONV - 1) + j, tm), :] * scw[j:j + 1, :]
        y = term if y is None else y + term
    ob = (gb_ref[...].astype(F32) * y).astype(BF16)
    o_ref[...] = h_ref[...] + _dot(oa_ref[...], wo_ref[0:HW, :]) + _dot(ob, wo_ref[HW:2 * HW, :])


def even_out(oa, proj, sc_conv_w, w_out, h, t, tm=512):
    n, d = h.shape
    hb = tm // SUBLANES
    halo = lambda col: (lambda i: (jnp.maximum(i * hb - 1, 0), col))
    const = lambda i: (0, 0)
    return pl.pallas_call(
        functools.partial(_evenout_body, tm=tm, tiles_per_seq=t // tm),
        grid=(n // tm,),
        in_specs=[pl.BlockSpec((tm, HW), lambda i: (i, 0)),
                  pl.BlockSpec((tm, HW), lambda i: (i, 4)),
                  pl.BlockSpec((tm, HW), lambda i: (i, 5)),
                  pl.BlockSpec((tm, HW), lambda i: (i, 6)),
                  pl.BlockSpec((SUBLANES, HW), halo(5)),
                  pl.BlockSpec((SUBLANES, HW), halo(6)),
                  pl.BlockSpec((SC_CONV, HW), const),
                  pl.BlockSpec((2 * HW, d), const, pipeline_mode=pl.Buffered(1)),
                  pl.BlockSpec((tm, d), lambda i: (i, 0))],
        out_specs=pl.BlockSpec((tm, d), lambda i: (i, 0)),
        out_shape=jax.ShapeDtypeStruct((n, d), F32),
        scratch_shapes=[pltpu.VMEM((tm + SUBLANES, HW), F32)],
        compiler_params=_cparams("parallel"),
        name="even_out",
    )(oa, proj, proj, proj, proj, proj, sc_conv_w, w_out, h)


def _hgrn_body(hq_ref, hi_ref, hg_ref, hf_ref, lb_ref, ng_ref, o_ref,
               st_ref, q_ref, k_ref, v_ref, b_ref, *, tt):
    @pl.when(pl.program_id(1) == 0)
    def _():
        st_ref[...] = jnp.zeros_like(st_ref)

    lb = lb_ref[...]
    hf = hf_ref[...]
    q_ref[...] = _silu(hq_ref[...].astype(F32))
    k_ref[...] = (1.0 - lb) * _sigmoid(-hf)
    v_ref[...] = hi_ref[...].astype(F32)
    lf = jnp.log(lb + (1.0 - lb) * _sigmoid(hf))

    r = _iota((UNIT, UNIT), 0)
    c = _iota((UNIT, UNIT), 1)
    tril_f = jnp.where(((r // CHUNK) == (c // CHUNK)) & (c <= r), 1.0, 0.0)
    for u in range(tt // UNIT):
        rows = slice(u * UNIT, (u + 1) * UNIT)
        b_ref[rows, :] = jnp.dot(tril_f, lf[rows], precision=HIGHEST, preferred_element_type=F32)

    ng = ng_ref[...]
    ones = jnp.ones((HEAD, HEAD), BF16)
    ti = _iota((SUB, HEAD), 0)
    ci = _iota((CHUNK, HEAD), 0)

    for u in range(tt // UNIT):
        base = u * UNIT
        for h in range(N_HEADS):
            cs = slice(h * HEAD, (h + 1) * HEAD)
            vt = v_ref[base:base + UNIT, cs].T.astype(BF16)
            for half in range(2):
                r0 = base + half * CHUNK
                rows = slice(r0, r0 + CHUNK)
                qc = q_ref[rows, cs]
                kc = k_ref[rows, cs]
                vc = v_ref[rows, cs]
                bc = b_ref[rows, cs]
                bl = b_ref[r0 + CHUNK - 1:r0 + CHUNK, cs]
                st = st_ref[h]
                o_inter = _dot_nt((qc * jnp.exp(bc)).astype(BF16), st.astype(BF16))
                kdec = kc * jnp.exp(bl - bc)
                zero = jnp.zeros_like(kdec)
                kpad = (jnp.concatenate([kdec, zero], axis=0) if half == 0
                        else jnp.concatenate([zero, kdec], axis=0)).astype(BF16)
                st_ref[h] = st * jnp.exp(bl) + _dot(vt, kpad)
                vcb = vc.astype(BF16)
                parts = []
                for sb in range(CHUNK // SUB):
                    s0 = sb * SUB
                    qi = qc[s0:s0 + SUB]
                    bi = bc[s0:s0 + SUB]
                    oi = None
                    if sb > 0:
                        ref = b_ref[r0 + s0 - 1:r0 + s0, cs]
                        before = ci < s0
                        qs = (qi * jnp.exp(bi - ref)).astype(BF16)
                        ks = jnp.where(before, kc * jnp.exp(jnp.where(before, ref - bc, 0.0)), 0.0).astype(BF16)
                        att = _dot_nt(qs, ks)
                        oi = _dot(att.astype(BF16), vcb)
                    ps = []
                    for s in range(SUB):
                        rr = r0 + s0 + s
                        m = ti >= s
                        bs = b_ref[rr:rr + 1, cs]
                        ks_row = k_ref[rr:rr + 1, cs]
                        e = jnp.exp(jnp.where(m, bi - bs, 0.0))
                        ps.append(jnp.where(m, qi * e * ks_row, 0.0))
                    rs = _dot(jnp.concatenate(ps, axis=0).astype(BF16), ones)
                    for s in range(SUB):
                        rr = r0 + s0 + s
                        term = rs[s * SUB:(s + 1) * SUB] * v_ref[rr:rr + 1, cs]
                        oi = term if oi is None else oi + term
                    parts.append(oi)
                o = o_inter + jnp.concatenate(parts, axis=0)
                on = _rms(o, ng) * _sigmoid(hg_ref[rows, cs].astype(F32))
                o_ref[rows, cs] = on.astype(BF16)


def hgrn_scan(proj, hf, lb, ng, bsz, t, tt=512):
    n = proj.shape[0]
    nt = t // tt
    col = lambda cidx: (lambda b, i: (b * nt + i, cidx))
    const = lambda b, i: (0, 0)
    return pl.pallas_call(
        functools.partial(_hgrn_body, tt=tt),
        grid=(bsz, nt),
        in_specs=[pl.BlockSpec((tt, HW), col(0)), pl.BlockSpec((tt, HW), col(1)), pl.BlockSpec((tt, HW), col(2)),
                  pl.BlockSpec((tt, HW), col(0)),
                  pl.BlockSpec((1, HW), const), pl.BlockSpec((1, HEAD), const)],
        out_specs=pl.BlockSpec((tt, HW), col(0)),
        out_shape=jax.ShapeDtypeStruct((n, HW), BF16),
        scratch_shapes=[pltpu.VMEM((N_HEADS, HEAD, HEAD), F32)] + [pltpu.VMEM((tt, HW), F32)] * 4,
        compiler_params=_cparams("parallel", "arbitrary"),
        name="hgrn_scan",
    )(proj, proj, proj, hf, lb, ng)


def _swa_body(q_ref, k_ref, v_ref, kp_ref, vp_ref, cos_ref, sin_ref, cosp_ref, sinp_ref,
              o_ref, lse_ref, qs_ref, ks_ref, kps_ref, *, tq):
    i = pl.program_id(2)
    no_prev = jnp.where(i > 0, 0, SPAN + 1)
    cos = cos_ref[...]
    sin = sin_ref[...]
    cosp = cosp_ref[...]
    sinp = sinp_ref[...]

    def rope(x, cs_, sn_):
        return x * cs_ + pltpu.roll(x, HEAD // 2, axis=1) * sn_

    for h in range(N_HEADS):
        cs = slice(h * HEAD, (h + 1) * HEAD)
        qs_ref[:, cs] = rope(q_ref[0, :, cs].astype(F32), cos, sin).astype(BF16)
        ks_ref[:, cs] = rope(k_ref[0, :, cs].astype(F32), cos, sin).astype(BF16)
        kps_ref[:, cs] = rope(kp_ref[0, :, cs].astype(F32), cosp, sinp).astype(BF16)

    qi = _iota((SPAN, SPAN), 0)
    kj = _iota((SPAN, SPAN), 1)
    m_cur = kj <= qi
    m_prev_rest = kj >= qi
    m_prev_first = kj >= qi + no_prev
    lane = _iota((SPAN, HEAD), 1)
    scale = HEAD ** -0.5
    for n in range(tq // SPAN):
        rows = slice(n * SPAN, (n + 1) * SPAN)
        prow = slice((n - 1) * SPAN, n * SPAN)
        lse_tile = jnp.zeros((SPAN, HEAD), F32)
        for h in range(N_HEADS):
            cs = slice(h * HEAD, (h + 1) * HEAD)
            qn = qs_ref[rows, cs]
            kc = ks_ref[rows, cs]
            vc = v_ref[0, rows, cs]
            if n == 0:
                kp, vp, m_prev = kps_ref[:, cs], vp_ref[0, :, cs], m_prev_first
            else:
                kp, vp, m_prev = ks_ref[prow, cs], v_ref[0, prow, cs], m_prev_rest
            s_c = jnp.where(m_cur, _dot_nt(qn, kc) * scale, MASK_VALUE)
            s_p = jnp.where(m_prev, _dot_nt(qn, kp) * scale, MASK_VALUE)
            m = jnp.maximum(jnp.max(s_c, axis=-1, keepdims=True), jnp.max(s_p, axis=-1, keepdims=True))
            p_c = jnp.where(m_cur, jnp.exp(s_c - m), 0.0)
            p_p = jnp.where(m_prev, jnp.exp(s_p - m), 0.0)
            l = jnp.sum(p_c, axis=-1, keepdims=True) + jnp.sum(p_p, axis=-1, keepdims=True)
            o = (_dot(p_c.astype(BF16), vc) + _dot(p_p.astype(BF16), vp)) / l
            o_ref[0, rows, cs] = o.astype(BF16)
            lse_tile = jnp.where(lane == h, m + jnp.log(l), lse_tile)
        lse_ref[0, rows, :] = lse_tile


def swa_pattern(proj3, cos_t, sin_t, p, dilation, bsz, t):
    ncol = proj3.shape[2] // HW
    length = t // dilation
    tq = min(length, 1024)
    nq = length // tq
    pb = tq // SPAN
    xv = proj3.reshape(bsz, length, dilation * ncol * HW)
    cosv = cos_t.reshape(length, dilation * HEAD)
    sinv = sin_t.reshape(length, dilation * HEAD)
    cur = lambda cidx: (lambda b, r, i: (b, i, r * ncol + cidx))
    prev = lambda cidx: (lambda b, r, i: (b, jnp.maximum(i * pb - 1, 0), r * ncol + cidx))
    o, lse = pl.pallas_call(
        functools.partial(_swa_body, tq=tq),
        grid=(bsz, dilation, nq),
        in_specs=[pl.BlockSpec((1, tq, HW), cur(3 + p)),
                  pl.BlockSpec((1, tq, HW), cur(6 + p)),
                  pl.BlockSpec((1, tq, HW), cur(9 + p)),
                  pl.BlockSpec((1, SPAN, HW), prev(6 + p)),
                  pl.BlockSpec((1, SPAN, HW), prev(9 + p)),
                  pl.BlockSpec((tq, HEAD), lambda b, r, i: (i, r)),
                  pl.BlockSpec((tq, HEAD), lambda b, r, i: (i, r)),
                  pl.BlockSpec((SPAN, HEAD), lambda b, r, i: (jnp.maximum(i * pb - 1, 0), r)),
                  pl.BlockSpec((SPAN, HEAD), lambda b, r, i: (jnp.maximum(i * pb - 1, 0), r))],
        out_specs=[pl.BlockSpec((1, tq, HW), lambda b, r, i: (b, i, r)),
                   pl.BlockSpec((1, tq, HEAD), lambda b, r, i: (b, i, r))],
        out_shape=[jax.ShapeDtypeStruct((bsz, length, dilation * HW), BF16),
                   jax.ShapeDtypeStruct((bsz, length, dilation * HEAD), F32)],
        scratch_shapes=[pltpu.VMEM((tq, HW), BF16), pltpu.VMEM((tq, HW), BF16), pltpu.VMEM((SPAN, HW), BF16)],
        compiler_params=_cparams("parallel", "parallel", "parallel"),
        name="swa_d%d" % dilation,
    )(xv, xv, xv, xv, xv, cosv, sinv, cosv, sinv)
    return o.reshape(bsz * t, HW), lse.reshape(bsz * t, HEAD)


def _oddout_body(oc_ref, o0_ref, o1_ref, o2_ref, l0_ref, l1_ref, l2_ref, wo_ref, h_ref, o_ref):
    l0, l1, l2 = l0_ref[...], l1_ref[...], l2_ref[...]
    m = jnp.maximum(jnp.maximum(l0, l1), l2)
    e0, e1, e2 = jnp.exp(l0 - m), jnp.exp(l1 - m), jnp.exp(l2 - m)
    den = e0 + e1 + e2
    w0, w1, w2 = e0 / den, e1 / den, e2 / den
    acc = h_ref[...] + _dot(oc_ref[...], wo_ref[0:HW, :])
    for h in range(N_HEADS):
        cs = slice(h * HEAD, (h + 1) * HEAD)
        od = (o0_ref[:, cs].astype(F32) * _col(w0, h) + o1_ref[:, cs].astype(F32) * _col(w1, h)
              + o2_ref[:, cs].astype(F32) * _col(w2, h))
        acc = acc + _dot(od.astype(BF16), wo_ref[HW + h * HEAD:HW + (h + 1) * HEAD, :])
    o_ref[...] = acc


def odd_out(oc, outs, lses, w_out, h, tm=512):
    n, d = h.shape
    const = lambda i: (0, 0)
    row = lambda i: (i, 0)
    return pl.pallas_call(
        _oddout_body,
        grid=(n // tm,),
        in_specs=[pl.BlockSpec((tm, HW), row)] * 4 + [pl.BlockSpec((tm, HEAD), row)] * 3
                 + [pl.BlockSpec((2 * HW, d), const, pipeline_mode=pl.Buffered(1)),
                    pl.BlockSpec((tm, d), row)],
        out_specs=pl.BlockSpec((tm, d), row),
        out_shape=jax.ShapeDtypeStruct((n, d), F32),
        compiler_params=_cparams("parallel"),
        name="odd_out",
    )(oc, *outs, *lses, w_out, h)


def _pad_lanes(vec, offset):
    return jnp.zeros((1, HEAD), F32).at[0, offset:offset + vec.shape[0]].set(vec.astype(F32))


def layer_ab(h, bsz, t, norm_g, w_in, conv_w, a_log, dt_bias, dn_norm_g, sc_conv_w, w_out):
    o_ba = 4 * HW
    w_main = jnp.concatenate([w_in[:, :o_ba], w_in[:, o_ba + 2 * N_HEADS:]], axis=1).astype(BF16)
    w_ba = jnp.pad(w_in[:, o_ba:o_ba + 2 * N_HEADS], ((0, 0), (0, HEAD - 2 * N_HEADS))).astype(BF16)
    proj = norm_matmul(h, norm_g, w_main, BF16, tm=1024, tn=HW)
    ba = norm_matmul(h, norm_g, w_ba, F32, tm=1024, tn=HEAD)
    ub, w, qd, kdt, qk, gt = delta_prep(proj, ba, conv_w.astype(F32), _pad_lanes(a_log, N_HEADS),
                                        _pad_lanes(dt_bias, N_HEADS), bsz, t)
    oa = delta_scan(ub, w, qd, kdt, qk, gt, proj, dn_norm_g.reshape(1, HEAD).astype(F32), bsz, t)
    return even_out(oa, proj, sc_conv_w.astype(F32), w_out.astype(BF16), h, t)


def layer_cd(h, bsz, t, norm_g, w_in, lower_bound, hg_norm_g, w_out, cos_t, sin_t):
    w_main = jnp.concatenate([w_in[:, :HW], w_in[:, 2 * HW:]], axis=1).astype(BF16)
    w_f = w_in[:, HW:2 * HW].astype(BF16)
    proj = norm_matmul(h, norm_g, w_main, BF16, tm=1024, tn=HW)
    hf = norm_matmul(h, norm_g, w_f, F32, tm=1024, tn=HW)
    oc = hgrn_scan(proj, hf, lower_bound.reshape(1, HW).astype(F32), hg_norm_g.reshape(1, HEAD).astype(F32), bsz, t)
    proj3 = proj.reshape(bsz, t, proj.shape[1])
    outs, lses = [], []
    for p, (window, dilation) in enumerate(SWA_PATTERNS):
        o_p, lse_p = swa_pattern(proj3, cos_t, sin_t, p, dilation, bsz, t)
        outs.append(o_p)
        lses.append(lse_p)
    return odd_out(oc, outs, lses, w_out.astype(BF16), h)


def _rope_tables(t):
    half = HEAD // 2
    inv_freq = ROPE_THETA ** (-jnp.arange(half, dtype=F32) * 2.0 / HEAD)
    ang = jnp.arange(t, dtype=F32)[:, None] * inv_freq[None, :]
    cos, sin = jnp.cos(ang), jnp.sin(ang)
    return jnp.concatenate([cos, cos], axis=1), jnp.concatenate([-sin, sin], axis=1)


def kernel(x, norm_mix_g, norm_ffn_g, norm_final_g, ab_w_in, dn_conv_w, dn_a_log, dn_dt_bias, dn_norm_g,
           sc_conv_w, ab_w_out, cd_w_in, hg_lower_bounds, hg_norm_g, cd_w_out, ffn_w_gate, ffn_w_up, ffn_w_down):
    bsz, t, d = x.shape
    depth = norm_mix_g.shape[0]
    sm = jax.nn.softmax(hg_lower_bounds.astype(F32), axis=0)
    lower_bounds = jnp.cumsum(sm, axis=0) - sm[0]
    cos_t, sin_t = _rope_tables(t)
    h = x.reshape(bsz * t, d)
    for layer in range(depth):
        i = layer // 2
        if layer % 2 == 0:
            h = layer_ab(h, bsz, t, norm_mix_g[layer], ab_w_in[i], dn_conv_w[i], dn_a_log[i], dn_dt_bias[i],
                         dn_norm_g[i], sc_conv_w[i], ab_w_out[i])
        else:
            h = layer_cd(h, bsz, t, norm_mix_g[layer], cd_w_in[i], lower_bounds[i], hg_norm_g[i], cd_w_out[i],
                         cos_t, sin_t)
        h = ffn(h, norm_ffn_g[layer], ffn_w_gate[layer].astype(BF16), ffn_w_up[layer].astype(BF16),
                ffn_w_down[layer].astype(BF16), norm_final_g, final=(layer == depth - 1))
    return h.reshape(bsz, t, d)
```
